```python
import math
import jax, jax.numpy as jnp
from jax import lax
import numpy as np

D_MODEL = 1024
BATCH = 32
SEQ = 2048
DEPTH = 2

CHUNK = 64
N_META = 16
PAD = 2 * CHUNK - N_META
LEAD = PAD + N_META
Q_BLOCK = 128
N_A_LAYERS = DEPTH // 2
N_B_LAYERS = DEPTH - N_A_LAYERS
EPS = 1e-6
NEG = -1e30

DN_HEADS = D_MODEL // 128
DN_DK = 128
DN_DV = 128
DN_KW = DN_HEADS * DN_DK
DN_VW = DN_HEADS * DN_DV
CONV_K = 4

MLA_HEADS = D_MODEL // 128
QK_NOPE = 128
QK_ROPE = 64
QK_DIM = QK_NOPE + QK_ROPE
V_HEAD = 128
MLA_VW = MLA_HEADS * V_HEAD
KV_RANK = D_MODEL // 4
Q_RANK = 3 * D_MODEL // 8
ROPE_THETA = 10000.0

kernel_name = "yoco_gdn_mla_hybrid"


def rms_norm(x, g):
    xf = x.astype(jnp.float32)
    y = xf * lax.rsqrt(jnp.mean(xf * xf, -1, keepdims=True) + EPS)
    return (y * g.astype(jnp.float32)).astype(x.dtype)


def l2_norm(x):
    xf = x.astype(jnp.float32)
    return (xf * lax.rsqrt(jnp.sum(xf * xf, -1, keepdims=True) + EPS)).astype(x.dtype)


def causal_dwconv(x, w):
    c = x.shape[-1]
    return lax.conv_general_dilated(
        x, w[:, None, :].astype(x.dtype), window_strides=(1,),
        padding=[(w.shape[0] - 1, 0)], dimension_numbers=('NWC', 'WIO', 'NWC'),
        feature_group_count=c)


def rope(x, pos):
    half = x.shape[-1] // 2
    inv = ROPE_THETA ** (-jnp.arange(half, dtype=jnp.float32) / half)
    ang = pos.astype(jnp.float32)[:, None] * inv[None, :]
    cos = jnp.cos(ang)[None, :, None, :]
    sin = jnp.sin(ang)[None, :, None, :]
    x1 = x[..., :half].astype(jnp.float32)
    x2 = x[..., half:].astype(jnp.float32)
    return jnp.concatenate([x1 * cos - x2 * sin, x2 * cos + x1 * sin], -1).astype(x.dtype)


def gated_delta_rule(q, k, v, g, beta):
    out_dtype = v.dtype
    q, k, v = q.astype(jnp.float32), k.astype(jnp.float32), v.astype(jnp.float32)
    bsz, nh, _, c, dk = q.shape
    dv = v.shape[-1]
    gc = jnp.cumsum(g, -1)
    tril = jnp.tril(jnp.ones((c, c), bool))
    tril_strict = jnp.tril(jnp.ones((c, c), bool), -1)
    decay = jnp.exp(jnp.where(tril, gc[..., :, None] - gc[..., None, :], -jnp.inf))
    kb = k * beta[..., None]
    lower = jnp.where(tril_strict, jnp.einsum('bhnid,bhnjd->bhnij', kb, k) * decay, 0.0)
    system = lower + jnp.eye(c, dtype=jnp.float32)
    rhs = jnp.concatenate([v * beta[..., None], kb * jnp.exp(gc)[..., None]], -1)
    sol = lax.linalg.triangular_solve(system, rhs, left_side=True, lower=True)
    u_base, w_dec = sol[..., :dv], sol[..., dv:]
    attn_intra = jnp.where(tril, jnp.einsum('bhnid,bhnjd->bhnij', q, k) * decay, 0.0)
    q_dec = q * jnp.exp(gc)[..., None]
    k_dec = k * jnp.exp(gc[..., -1:] - gc)[..., None]
    g_last = jnp.exp(gc[..., -1])

    def step(state, xs):
        u0, wd, qd, att, kd, gl = xs
        u = u0 - jnp.einsum('bhck,bhkv->bhcv', wd, state)
        o = jnp.einsum('bhck,bhkv->bhcv', qd, state) + jnp.einsum('bhij,bhjv->bhiv', att, u)
        state = state * gl[..., None, None] + jnp.einsum('bhck,bhcv->bhkv', kd, u)
        return state, o

    xs = tuple(jnp.moveaxis(a, 2, 0) for a in (u_base, w_dec, q_dec, attn_intra, k_dec, g_last))
    s0 = jnp.zeros((bsz, nh, dk, dv), jnp.float32)
    _, o = lax.scan(step, s0, xs)
    return jnp.moveaxis(o, 0, 2).astype(out_dtype)


def deltanet_layer(x, valid, g_norm, w_in, conv_w, a_log, dt_bias, o_gain, w_out):
    bsz, lp, _ = x.shape
    n = lp // CHUNK
    h = jnp.where(valid[None, :, None], rms_norm(x, g_norm), 0)
    z = h @ w_in
    qkv = jax.nn.silu(causal_dwconv(z[..., :2 * DN_KW + DN_VW], conv_w))
    gate = z[..., 2 * DN_KW + DN_VW:2 * DN_KW + 2 * DN_VW]
    b_raw = z[..., 2 * DN_KW + 2 * DN_VW:2 * DN_KW + 2 * DN_VW + DN_HEADS].astype(jnp.float32)
    a_raw = z[..., 2 * DN_KW + 2 * DN_VW + DN_HEADS:].astype(jnp.float32)
    q = qkv[..., :DN_KW].reshape(bsz, lp, DN_HEADS, DN_DK)
    k = qkv[..., DN_KW:2 * DN_KW].reshape(bsz, lp, DN_HEADS, DN_DK)
    v = qkv[..., 2 * DN_KW:].reshape(bsz, lp, DN_HEADS, DN_DV)
    q = l2_norm(q) * (DN_DK ** -0.5)
    k = l2_norm(k)
    beta = jax.nn.sigmoid(b_raw)
    g = -jnp.exp(a_log.astype(jnp.float32)) * jax.nn.softplus(a_raw + dt_bias.astype(jnp.float32))

    def to_chunks(a):
        return a.reshape(bsz, n, CHUNK, DN_HEADS, -1).transpose(0, 3, 1, 2, 4)

    def gate_chunks(a):
        return a.reshape(bsz, n, CHUNK, DN_HEADS).transpose(0, 3, 1, 2)

    o = gated_delta_rule(to_chunks(q), to_chunks(k), to_chunks(v), gate_chunks(g), gate_chunks(beta))
    o = o.transpose(0, 2, 3, 1, 4).reshape(bsz, lp, DN_HEADS, DN_DV)
    o = rms_norm(o, o_gain) * jax.nn.silu(gate).reshape(bsz, lp, DN_HEADS, DN_DV)
    return x + o.reshape(bsz, lp, DN_VW) @ w_out


def shared_latent_kv(x, pos, g_norm, w_down, g_latent, w_uk, w_uv, k_gain):
    bsz, lp, _ = x.shape
    h = rms_norm(x, g_norm)
    c = h @ w_down
    c_kv = rms_norm(c[..., :KV_RANK], g_latent)
    k_pe = jnp.broadcast_to(c[..., None, KV_RANK:], (bsz, lp, MLA_HEADS, QK_ROPE))
    k_nope = (c_kv @ w_uk).reshape(bsz, lp, MLA_HEADS, QK_NOPE)
    v = (c_kv @ w_uv).reshape(bsz, lp, MLA_HEADS, V_HEAD)
    k = rms_norm(jnp.concatenate([k_nope, k_pe], -1), k_gain)
    k = jnp.concatenate([k[..., :QK_NOPE], rope(k[..., QK_NOPE:], pos)], -1)
    return k, v


def chunk_causal_attention(q, k, v):
    bsz, lp, nh, d = q.shape
    nb = lp // Q_BLOCK
    scale = d ** -0.5
    key_pos = jnp.arange(lp)
    key_chunk = key_pos // CHUNK
    key_ok = key_pos >= PAD
    kf = k.astype(jnp.float32)
    vf = v.astype(jnp.float32)
    qb = q.reshape(bsz, nb, Q_BLOCK, nh, d).transpose(1, 0, 2, 3, 4)

    def one_block(args):
        q_blk, i = args
        q_chunk = (i * Q_BLOCK + jnp.arange(Q_BLOCK)) // CHUNK
        mask = (key_chunk[None, :] <= q_chunk[:, None]) & key_ok[None, :]
        s = jnp.einsum('bqhd,bkhd->bhqk', q_blk.astype(jnp.float32), kf) * scale
        p = jax.nn.softmax(jnp.where(mask[None, None], s, NEG), -1)
        return jnp.einsum('bhqk,bkhd->bqhd', p, vf).astype(v.dtype)

    o = lax.map(one_block, (qb, jnp.arange(nb)))
    return o.transpose(1, 0, 2, 3, 4).reshape(bsz, lp, nh, v.shape[-1])


def mla_layer(x, pos, k, v, g_norm, w_in, g_q_latent, w_uq, q_gain, w_out):
    bsz, lp, _ = x.shape
    h = rms_norm(x, g_norm)
    z = h @ w_in
    c_q = rms_norm(z[..., :Q_RANK], g_q_latent)
    gate = z[..., Q_RANK:]
    q = rms_norm((c_q @ w_uq).reshape(bsz, lp, MLA_HEADS, QK_DIM), q_gain)
    q = jnp.concatenate([q[..., :QK_NOPE], rope(q[..., QK_NOPE:], pos)], -1)
    o = chunk_causal_attention(q, k, v).reshape(bsz, lp, MLA_VW) * jax.nn.silu(gate)
    return x + o @ w_out


def setup_inputs(seed: int = 0) -> dict:
    key = jax.random.key(seed)
    ks = jax.random.split(key, 24)
    f32 = jnp.float32

    def nrm(k, shape, fan_in):
        return jax.random.normal(k, shape, f32) * (fan_in ** -0.5)

    def gain(k, shape):
        return 1.0 + 0.01 * jax.random.normal(k, shape, f32)

    a_in_w = 2 * DN_KW + 2 * DN_VW + 2 * DN_HEADS
    dt = jnp.exp(jax.random.uniform(ks[5], (N_A_LAYERS, DN_HEADS), f32) * (math.log(0.1) - math.log(0.001)) + math.log(0.001))
    return {
        "x": jax.random.normal(ks[0], (BATCH, SEQ, D_MODEL), f32),
        "meta_tokens": jax.random.normal(ks[1], (N_META, D_MODEL), f32),
        "a_norm": gain(ks[2], (N_A_LAYERS, D_MODEL)),
        "a_w_in": nrm(ks[3], (N_A_LAYERS, D_MODEL, a_in_w), D_MODEL),
        "a_conv": nrm(ks[4], (N_A_LAYERS, CONV_K, 2 * DN_KW + DN_VW), CONV_K),
        "a_log": jnp.log(jax.random.uniform(ks[6], (N_A_LAYERS, DN_HEADS), f32, 1.0, 16.0)),
        "a_dt_bias": dt + jnp.log(-jnp.expm1(-dt)),
        "a_o_gain": gain(ks[7], (N_A_LAYERS, DN_DV)),
        "a_w_out": nrm(ks[8], (N_A_LAYERS, DN_VW, D_MODEL), DN_VW),
        "kv_norm": gain(ks[9], (D_MODEL,)),
        "kv_w_down": nrm(ks[10], (D_MODEL, KV_RANK + QK_ROPE), D_MODEL),
        "kv_latent_norm": gain(ks[11], (KV_RANK,)),
        "kv_w_uk": nrm(ks[12], (KV_RANK, MLA_HEADS * QK_NOPE), KV_RANK),
        "kv_w_uv": nrm(ks[13], (KV_RANK, MLA_VW), KV_RANK),
        "k_gain": gain(ks[14], (QK_DIM,)),
        "b_norm": gain(ks[15], (N_B_LAYERS, D_MODEL)),
        "b_w_in": nrm(ks[16], (N_B_LAYERS, D_MODEL, Q_RANK + MLA_VW), D_MODEL),
        "b_q_latent_norm": gain(ks[17], (N_B_LAYERS, Q_RANK)),
        "b_w_uq": nrm(ks[18], (N_B_LAYERS, Q_RANK, MLA_HEADS * QK_DIM), Q_RANK),
        "b_q_gain": gain(ks[19], (N_B_LAYERS, QK_DIM)),
        "b_w_out": nrm(ks[20], (N_B_LAYERS, MLA_VW, D_MODEL), MLA_VW),
    }


def reference(x, meta_tokens, a_norm, a_w_in, a_conv, a_log, a_dt_bias, a_o_gain, a_w_out,
              kv_norm, kv_w_down, kv_latent_norm, kv_w_uk, kv_w_uv, k_gain,
              b_norm, b_w_in, b_q_latent_norm, b_w_uq, b_q_gain, b_w_out):
    bsz = x.shape[0]
    pad = jnp.zeros((bsz, PAD, D_MODEL), x.dtype)
    meta = jnp.broadcast_to(meta_tokens.astype(x.dtype)[None], (bsz, N_META, D_MODEL))
    h = jnp.concatenate([pad, meta, x], 1)
    lp = h.shape[1]
    p = jnp.arange(lp)
    valid = p >= PAD
    pos = jnp.maximum(p - PAD, 0)
    k_sh = None
    v_sh = None
    for i in range(DEPTH):
        if i < N_A_LAYERS:
            h = deltanet_layer(h, valid, a_norm[i], a_w_in[i], a_conv[i], a_log[i],
                               a_dt_bias[i], a_o_gain[i], a_w_out[i])
            if i == N_A_LAYERS - 1:
                k_sh, v_sh = shared_latent_kv(h, pos, kv_norm, kv_w_down, kv_latent_norm,
                                              kv_w_uk, kv_w_uv, k_gain)
        else:
            j = i - N_A_LAYERS
            h = mla_layer(h, pos, k_sh, v_sh, b_norm[j], b_w_in[j], b_q_latent_norm[j],
                          b_w_uq[j], b_q_gain[j], b_w_out[j])
    return h[:, LEAD:]
```

```python
import functools
import math

import jax
import jax.numpy as jnp
from jax import lax
from jax.experimental import pallas as pl
from jax.experimental.pallas import tpu as pltpu

F32 = jnp.float32
BF16 = jnp.bfloat16

D_MODEL = 1024
CHUNK = 64
N_META = 16
FRONT = 128
PAD = 2 * CHUNK - N_META + FRONT
LEAD = PAD + N_META
EPS = 1e-6
NEG = -1e30

HEADS = D_MODEL // 128
HEAD_DIM = 128
CONV_K = 4
CONV_W = 3 * D_MODEL
A_IN_W = 4 * D_MODEL + 2 * HEADS
QK_ROPE = 64
QK_DIM = HEAD_DIM + QK_ROPE
KV_RANK = D_MODEL // 4
Q_RANK = 3 * D_MODEL // 8
ROPE_THETA = 10000.0

LANES = 128
SUBLANES = 8
QK_PAD = 2 * LANES
ROW_TILE = 128
Q_TILE = 256
K_TILE = 256
VMEM_LIMIT = 56 * 1024 * 1024


def _dot(a, b):
    return jnp.dot(a.astype(BF16), b.astype(BF16), preferred_element_type=F32)


def _dot_nt(a, b):
    return lax.dot_general(a.astype(BF16), b.astype(BF16), (((1,), (1,)), ((), ())),
                           preferred_element_type=F32)


def _dot_tn(a, b):
    return lax.dot_general(a.astype(BF16), b.astype(BF16), (((0,), (0,)), ((), ())),
                           preferred_element_type=F32)


def _sigmoid(x):
    return 1.0 / (1.0 + jnp.exp(-x))


def _silu(x):
    return x * _sigmoid(x)


def _softplus(x):
    return jnp.maximum(x, 0.0) + jnp.log(1.0 + jnp.exp(-jnp.abs(x)))


def _gdn_in_kernel(h_ref, gn_ref, w_ref, cw_ref, ab_ref, q_ref, k_ref, v_ref, sg_ref, bg_ref, zbuf):
    t = pl.program_id(1)
    tm = h_ref.shape[1]
    x = h_ref[0]
    row = t * tm + lax.broadcasted_iota(jnp.int32, (tm, 1), 0)
    hn = x * lax.rsqrt(jnp.mean(x * x, axis=-1, keepdims=True) + EPS) * gn_ref[...]
    hn = jnp.where(row >= PAD, hn, 0.0)
    z = _dot(hn, w_ref[...])

    @pl.when(t == 0)
    def _():
        zbuf[0:SUBLANES, :] = jnp.zeros((SUBLANES, CONV_W), F32)

    zbuf[SUBLANES:SUBLANES + tm, :] = z[:, :CONV_W]
    y = None
    for j in range(CONV_K):
        off = SUBLANES - (CONV_K - 1) + j
        term = zbuf[off:off + tm, :] * cw_ref[j:j + 1, :]
        y = term if y is None else y + term
    zbuf[0:SUBLANES, :] = zbuf[tm:tm + SUBLANES, :]
    qkv = _silu(y)

    scale = HEAD_DIM ** -0.5
    for hd in range(HEADS):
        lo, hi = hd * HEAD_DIM, (hd + 1) * HEAD_DIM
        qh = qkv[:, lo:hi]
        kh = qkv[:, D_MODEL + lo:D_MODEL + hi]
        q_ref[0, :, lo:hi] = (qh * (lax.rsqrt(jnp.sum(qh * qh, -1, keepdims=True) + EPS) * scale)).astype(BF16)
        k_ref[0, :, lo:hi] = (kh * lax.rsqrt(jnp.sum(kh * kh, -1, keepdims=True) + EPS)).astype(BF16)
    v_ref[0] = qkv[:, 2 * D_MODEL:].astype(BF16)
    sg_ref[0] = _silu(z[:, CONV_W:CONV_W + D_MODEL]).astype(BF16)

    raw = z[:, 4 * D_MODEL:]
    lane = lax.broadcasted_iota(jnp.int32, raw.shape, 1)
    beta = _sigmoid(raw)
    g = -jnp.exp(ab_ref[0:1, :]) * _softplus(raw + ab_ref[1:2, :])
    bg_ref[0] = jnp.where(lane < HEADS, beta, g)


def _gdn_in(h, a_norm, w_in, conv_w, ab):
    bsz, lp, _ = h.shape
    nt = lp // ROW_TILE
    zw = w_in.shape[1]
    tile = lambda w: pl.BlockSpec((1, ROW_TILE, w), lambda b, t: (b, t, 0))
    full = lambda a: pl.BlockSpec(a.shape, lambda b, t: (0,) * a.ndim)
    out_bf = jax.ShapeDtypeStruct((bsz, lp, D_MODEL), BF16)
    return pl.pallas_call(
        _gdn_in_kernel,
        grid=(bsz, nt),
        in_specs=[tile(D_MODEL), full(a_norm), full(w_in), full(conv_w), full(ab)],
        out_specs=[tile(D_MODEL)] * 4 + [tile(LANES)],
        out_shape=[out_bf] * 4 + [jax.ShapeDtypeStruct((bsz, lp, LANES), F32)],
        scratch_shapes=[pltpu.VMEM((ROW_TILE + SUBLANES, CONV_W), F32)],
        compiler_params=pltpu.CompilerParams(dimension_semantics=("arbitrary", "arbitrary"),
                                             vmem_limit_bytes=VMEM_LIMIT),
        name="gdn_in",
    )(h, a_norm, w_in, conv_w, ab)


def _unit_lower_inverse(lm, row, col):
    n = lm.shape[0]
    eye = (row == col).astype(F32)
    b0 = 16
    same = (row // b0) == (col // b0)
    nd = jnp.where(same, -lm, 0.0)
    n2 = _dot(nd, nd)
    n4 = _dot(n2, n2)
    n8 = _dot(n4, n4)
    p = eye + nd
    p = p + _dot(p, n2)
    p = p + _dot(p, n4)
    p = p + _dot(p, n8)
    b = b0
    while b < n:
        same2 = (row // (2 * b)) == (col // (2 * b))
        off = jnp.where(jnp.logical_and(same2, jnp.logical_not(same)), lm, 0.0)
        p = p - _dot(p, _dot(off, p))
        same = same2
        b *= 2
    return p


def _gdn_core_kernel(q_ref, k_ref, v_ref, sg_ref, bg_ref, h_ref, og_ref, wo_ref, out_ref, s_ref, o_buf):
    t = pl.program_id(1)
    tm = q_ref.shape[1]
    c = CHUNK

    @pl.when(t == 0)
    def _():
        s_ref[...] = jnp.zeros(s_ref.shape, F32)

    bg = bg_ref[0]
    rt = lax.broadcasted_iota(jnp.int32, (tm, tm), 0)
    ct = lax.broadcasted_iota(jnp.int32, (tm, tm), 1)
    ltri = jnp.logical_and(rt // c == ct // c, ct <= rt).astype(F32)
    gcum = jnp.dot(ltri, bg, preferred_element_type=F32, precision=lax.Precision.HIGHEST)
    gcum_t = gcum.T

    row = lax.broadcasted_iota(jnp.int32, (c, c), 0)
    col = lax.broadcasted_iota(jnp.int32, (c, c), 1)
    tril = col <= row
    strict = col < row

    for ci in range(tm // c):
        r0, r1 = ci * c, (ci + 1) * c
        for hd in range(HEADS):
            lo, hi = hd * HEAD_DIM, (hd + 1) * HEAD_DIM
            qh = q_ref[0, r0:r1, lo:hi].astype(F32)
            kh = k_ref[0, r0:r1, lo:hi].astype(F32)
            vh = v_ref[0, r0:r1, lo:hi].astype(F32)
            beta = bg[r0:r1, hd:hd + 1]
            gc = gcum[r0:r1, HEADS + hd:HEADS + hd + 1]
            gr = gcum_t[HEADS + hd:HEADS + hd + 1, r0:r1]
            gl = gc[c - 1:c, :]
            decay = jnp.where(tril, jnp.exp(jnp.minimum(gc - gr, 0.0)), 0.0)
            kb = kh * beta
            lm = jnp.where(strict, _dot_nt(kb, kh) * decay, 0.0)
            tinv = _unit_lower_inverse(lm, row, col)
            egc = jnp.exp(gc)
            sol = _dot(tinv, jnp.concatenate([vh * beta, kb * egc], axis=-1))
            u0, wd = sol[:, :HEAD_DIM], sol[:, HEAD_DIM:]
            att = jnp.where(tril, _dot_nt(qh, kh) * decay, 0.0)
            sh = s_ref[hd]
            u = u0 - _dot(wd, sh)
            o = _dot(qh * egc, sh) + _dot(att, u)
            s_ref[hd] = sh * jnp.exp(gl) + _dot_tn(kh * jnp.exp(gl - gc), u)
            o_n = o * lax.rsqrt(jnp.mean(o * o, -1, keepdims=True) + EPS) * og_ref[...]
            o_buf[r0:r1, lo:hi] = o_n * sg_ref[0, r0:r1, lo:hi].astype(F32)

    out_ref[0] = h_ref[0] + _dot(o_buf[...], wo_ref[...])


def _gdn_core(q, k, v, sg, bg, h, o_gain, w_out):
    bsz, lp, _ = h.shape
    nt = lp // ROW_TILE
    tile = lambda w: pl.BlockSpec((1, ROW_TILE, w), lambda b, t: (b, t, 0))
    full = lambda a: pl.BlockSpec(a.shape, lambda b, t: (0,) * a.ndim)
    return pl.pallas_call(
        _gdn_core_kernel,
        grid=(bsz, nt),
        in_specs=[tile(D_MODEL)] * 4 + [tile(LANES), tile(D_MODEL), full(o_gain), full(w_out)],
        out_specs=tile(D_MODEL),
        out_shape=jax.ShapeDtypeStruct((bsz, lp, D_MODEL), F32),
        scratch_shapes=[pltpu.VMEM((HEADS, HEAD_DIM, HEAD_DIM), F32),
                        pltpu.VMEM((ROW_TILE, D_MODEL), F32)],
        compiler_params=pltpu.CompilerParams(dimension_semantics=("arbitrary", "arbitrary"),
                                             vmem_limit_bytes=VMEM_LIMIT),
        name="gdn_core",
    )(q, k, v, sg, bg, h, o_gain, w_out)


def _mla_proj_kernel(h_ref, kvn_ref, wd_ref, lat_ref, wukv_ref, kg_ref, bn_ref, win_ref, qlat_ref, wuq_ref,
                     qg_ref, cs_ref, q_ref, k_ref, v_ref, sg_ref):
    x = h_ref[0]
    xn = x * lax.rsqrt(jnp.mean(x * x, axis=-1, keepdims=True) + EPS)
    cos = cs_ref[:, :LANES]
    sin = cs_ref[:, LANES:]

    c = _dot(xn * kvn_ref[...], wd_ref[...])
    lat = c[:, :KV_RANK]
    c_kv = lat * lax.rsqrt(jnp.mean(lat * lat, -1, keepdims=True) + EPS) * lat_ref[...]
    kv = _dot(c_kv, wukv_ref[...])
    v_ref[0] = kv[:, D_MODEL:].astype(BF16)
    pe = c[:, KV_RANK:KV_RANK + LANES]
    per = c[:, KV_RANK + LANES:KV_RANK + 2 * LANES]
    pe_ss = jnp.sum(pe * pe, -1, keepdims=True)
    pe_rot = pe * kg_ref[1:2, :] * cos + per * kg_ref[2:3, :] * sin
    for hd in range(HEADS):
        kn = kv[:, hd * HEAD_DIM:(hd + 1) * HEAD_DIM]
        r = lax.rsqrt((jnp.sum(kn * kn, -1, keepdims=True) + pe_ss) * (1.0 / QK_DIM) + EPS)
        k_ref[0, :, hd * QK_PAD:hd * QK_PAD + LANES] = (kn * r * kg_ref[0:1, :]).astype(BF16)
        k_ref[0, :, hd * QK_PAD + LANES:(hd + 1) * QK_PAD] = (pe_rot * r).astype(BF16)

    z = _dot(xn * bn_ref[...], win_ref[...])
    cq = z[:, :Q_RANK]
    c_q = cq * lax.rsqrt(jnp.mean(cq * cq, -1, keepdims=True) + EPS) * qlat_ref[...]
    sg_ref[0] = _silu(z[:, Q_RANK:]).astype(BF16)
    qq = _dot(c_q, wuq_ref[...])
    scale = QK_DIM ** -0.5
    for hd in range(HEADS):
        qn = qq[:, hd * QK_PAD:hd * QK_PAD + LANES]
        qp = qq[:, hd * QK_PAD + LANES:(hd + 1) * QK_PAD]
        qr = qq[:, HEADS * QK_PAD + hd * LANES:HEADS * QK_PAD + (hd + 1) * LANES]
        ss = jnp.sum(qn * qn, -1, keepdims=True) + jnp.sum(qp * qp, -1, keepdims=True)
        r = lax.rsqrt(ss * (1.0 / QK_DIM) + EPS) * scale
        q_ref[0, :, hd * QK_PAD:hd * QK_PAD + LANES] = (qn * r * qg_ref[0:1, :]).astype(BF16)
        q_ref[0, :, hd * QK_PAD + LANES:(hd + 1) * QK_PAD] = (
            (qp * qg_ref[1:2, :] * cos + qr * qg_ref[2:3, :] * sin) * r).astype(BF16)


def _mla_proj(h, kvn, wd, latn, wukv, kg, bn, win, qlat, wuq, qg, cs):
    bsz, lp, _ = h.shape
    nt = lp // ROW_TILE
    tile = lambda w: pl.BlockSpec((1, ROW_TILE, w), lambda b, t: (b, t, 0))
    full = lambda a: pl.BlockSpec(a.shape, lambda b, t: (0,) * a.ndim)
    params = (kvn, wd, latn, wukv, kg, bn, win, qlat, wuq, qg)
    sds = lambda w: jax.ShapeDtypeStruct((bsz, lp, w), BF16)
    return pl.pallas_call(
        _mla_proj_kernel,
        grid=(bsz, nt),
        in_specs=[tile(D_MODEL)] + [full(a) for a in params]
                 + [pl.BlockSpec((ROW_TILE, 2 * LANES), lambda b, t: (t, 0))],
        out_specs=[tile(HEADS * QK_PAD), tile(HEADS * QK_PAD), tile(D_MODEL), tile(D_MODEL)],
        out_shape=[sds(HEADS * QK_PAD), sds(HEADS * QK_PAD), sds(D_MODEL), sds(D_MODEL)],
        compiler_params=pltpu.CompilerParams(dimension_semantics=("arbitrary", "arbitrary"),
                                             vmem_limit_bytes=VMEM_LIMIT),
        name="mla_proj",
    )(h, *params, cs)


def _mla_attn_kernel(q_ref, k_ref, v_ref, sg_ref, h_ref, wo_ref, out_ref, m_ref, l_ref, acc_ref):
    qi = pl.program_id(1)
    tq = q_ref.shape[1]

    for hd in range(HEADS):
        qh = q_ref[0, :, hd * QK_PAD:(hd + 1) * QK_PAD]
        s = _dot_nt(qh, k_ref[0, PAD:LEAD, hd * QK_PAD:(hd + 1) * QK_PAD])
        m = jnp.max(s, -1, keepdims=True)
        p = jnp.exp(s - m)
        m_ref[:, hd:hd + 1] = m
        l_ref[:, hd:hd + 1] = jnp.sum(p, -1, keepdims=True)
        acc_ref[:, hd * HEAD_DIM:(hd + 1) * HEAD_DIM] = _dot(p, v_ref[0, PAD:LEAD, hd * HEAD_DIM:(hd + 1) * HEAD_DIM])

    def update(j, masked):
        k0 = pl.multiple_of(j * K_TILE, K_TILE)
        if masked:
            kc = lax.broadcasted_iota(jnp.int32, (tq, K_TILE), 1) // CHUNK
            qc = lax.broadcasted_iota(jnp.int32, (tq, K_TILE), 0) // CHUNK
            ok = kc <= qc
        for hd in range(HEADS):
            qh = q_ref[0, :, hd * QK_PAD:(hd + 1) * QK_PAD]
            kh = k_ref[0, pl.ds(k0, K_TILE), hd * QK_PAD:(hd + 1) * QK_PAD]
            vh = v_ref[0, pl.ds(k0, K_TILE), hd * HEAD_DIM:(hd + 1) * HEAD_DIM]
            s = _dot_nt(qh, kh)
            if masked:
                s = jnp.where(ok, s, NEG)
            m_old = m_ref[:, hd:hd + 1]
            m_new = jnp.maximum(m_old, jnp.max(s, -1, keepdims=True))
            alpha = jnp.exp(m_old - m_new)
            p = jnp.exp(s - m_new)
            l_ref[:, hd:hd + 1] = alpha * l_ref[:, hd:hd + 1] + jnp.sum(p, -1, keepdims=True)
            m_ref[:, hd:hd + 1] = m_new
            sl = slice(hd * HEAD_DIM, (hd + 1) * HEAD_DIM)
            acc_ref[:, sl] = alpha * acc_ref[:, sl] + _dot(p, vh)

    def body(j, carry):
        update(j, False)
        return carry

    lax.fori_loop(1, qi + 1, body, 0)
    update(qi + 1, True)

    for hd in range(HEADS):
        sl = slice(hd * HEAD_DIM, (hd + 1) * HEAD_DIM)
        acc_ref[:, sl] = acc_ref[:, sl] * (1.0 / l_ref[:, hd:hd + 1]) * sg_ref[0, :, sl].astype(F32)
    out_ref[0] = h_ref[0] + _dot(acc_ref[...], wo_ref[...])


def _mla_attn(q, k, v, sg, h, w_out, seq):
    bsz, lp, _ = h.shape
    nq = seq // Q_TILE
    lead_blocks = LEAD // Q_TILE
    qtile = lambda w: pl.BlockSpec((1, Q_TILE, w), lambda b, i: (b, i + lead_blocks, 0))
    whole = lambda w: pl.BlockSpec((1, lp, w), lambda b, i: (b, 0, 0))
    return pl.pallas_call(
        _mla_attn_kernel,
        grid=(bsz, nq),
        in_specs=[qtile(HEADS * QK_PAD), whole(HEADS * QK_PAD), whole(D_MODEL), qtile(D_MODEL), qtile(D_MODEL),
                  pl.BlockSpec(w_out.shape, lambda b, i: (0, 0))],
        out_specs=pl.BlockSpec((1, Q_TILE, D_MODEL), lambda b, i: (b, i, 0)),
        out_shape=jax.ShapeDtypeStruct((bsz, seq, D_MODEL), F32),
        scratch_shapes=[pltpu.VMEM((Q_TILE, LANES), F32), pltpu.VMEM((Q_TILE, LANES), F32),
                        pltpu.VMEM((Q_TILE, D_MODEL), F32)],
        compiler_params=pltpu.CompilerParams(dimension_semantics=("arbitrary", "arbitrary"),
                                             vmem_limit_bytes=VMEM_LIMIT),
        name="mla_attn",
    )(q, k, v, sg, h, w_out)


def _rot_cols(w):
    half = w.shape[-1] // 2
    return jnp.concatenate([-w[..., half:], w[..., :half]], -1)


def _swap_halves(g):
    half = g.shape[-1] // 2
    return jnp.concatenate([g[..., half:], g[..., :half]], -1)


def _pad_lanes(a, width=LANES):
    return jnp.pad(a, [(0, 0)] * (a.ndim - 1) + [(0, width - a.shape[-1])])


def kernel(x, meta_tokens, a_norm, a_w_in, a_conv, a_log, a_dt_bias, a_o_gain, a_w_out, kv_norm, kv_w_down,
           kv_latent_norm, kv_w_uk, kv_w_uv, k_gain, b_norm, b_w_in, b_q_latent_norm, b_w_uq, b_q_gain, b_w_out):
    bsz, seq, _ = x.shape
    assert a_w_in.shape[0] == 1 and b_w_in.shape[0] == 1
    lp = seq + LEAD
    h0 = jnp.concatenate([jnp.zeros((bsz, PAD, D_MODEL), x.dtype),
                          jnp.broadcast_to(meta_tokens.astype(x.dtype)[None], (bsz, N_META, D_MODEL)), x], 1)

    w_in = _pad_lanes(a_w_in[0], 4 * D_MODEL + LANES).astype(BF16)
    ab = jnp.zeros((SUBLANES, LANES), F32)
    ab = ab.at[0, HEADS:2 * HEADS].set(a_log[0]).at[1, HEADS:2 * HEADS].set(a_dt_bias[0])
    q, k, v, sg, bg = _gdn_in(h0, a_norm[0][None], w_in, a_conv[0], ab)
    h1 = _gdn_core(q, k, v, sg, bg, h0, a_o_gain[0][None], a_w_out[0].astype(BF16))

    w_pe = kv_w_down[:, KV_RANK:]
    wd = jnp.concatenate([kv_w_down[:, :KV_RANK], _pad_lanes(w_pe), _pad_lanes(_rot_cols(w_pe))], -1).astype(BF16)
    wukv = jnp.concatenate([kv_w_uk, kv_w_uv], -1).astype(BF16)
    kg = jnp.zeros((SUBLANES, LANES), F32)
    kg = kg.at[0].set(k_gain[:HEAD_DIM]).at[1, :QK_ROPE].set(k_gain[HEAD_DIM:])
    kg = kg.at[2, :QK_ROPE].set(_swap_halves(k_gain[HEAD_DIM:]))
    wuq3 = b_w_uq[0].reshape(Q_RANK, HEADS, QK_DIM)
    wuq_main = jnp.concatenate([wuq3[..., :HEAD_DIM], _pad_lanes(wuq3[..., HEAD_DIM:])], -1)
    wuq_rot = _pad_lanes(_rot_cols(wuq3[..., HEAD_DIM:]))
    wuq = jnp.concatenate([wuq_main.reshape(Q_RANK, HEADS * QK_PAD), wuq_rot.reshape(Q_RANK, HEADS * LANES)],
                          -1).astype(BF16)
    qg = jnp.zeros((SUBLANES, LANES), F32)
    qg = qg.at[0].set(b_q_gain[0, :HEAD_DIM]).at[1, :QK_ROPE].set(b_q_gain[0, HEAD_DIM:])
    qg = qg.at[2, :QK_ROPE].set(_swap_halves(b_q_gain[0, HEAD_DIM:]))

    pos = jnp.maximum(jnp.arange(lp) - PAD, 0).astype(F32)
    half = QK_ROPE // 2
    inv = ROPE_THETA ** (-jnp.arange(half, dtype=F32) / half)
    ang = pos[:, None] * inv[None, :]
    ang2 = jnp.concatenate([ang, ang], -1)
    cs = jnp.concatenate([_pad_lanes(jnp.cos(ang2)), _pad_lanes(jnp.sin(ang2))], -1)

    qb, kb, vb, sgb = _mla_proj(h1, kv_norm[None], wd, kv_latent_norm[None], wukv, kg, b_norm[0][None],
                                b_w_in[0].astype(BF16), b_q_latent_norm[0][None], wuq, qg, cs)
    return _mla_attn(qb, kb, vb, sgb, h1, b_w_out[0].astype(BF16), seq)
```
